```python
import jax, jax.numpy as jnp
from jax import lax
import numpy as np

D_MODEL = 2048
BATCH = 1
SEQ = 8192
DEPTH = 4

CHUNK = 64

D_MIX = D_MODEL
D_CONV = D_MIX // 2
D_LRU = D_MIX - D_CONV
CONV_WIDTH = 31
LRU_CONV_WIDTH = 4
LRU_HEADS = 8
LRU_HEAD_DIM = D_LRU // LRU_HEADS
LRU_C = 8.0
D_IN = 2 * D_CONV + D_CONV + D_LRU + D_LRU
RMS_EPS = 1e-6
LN_EPS = 1e-5

kernel_name = "hybrid_conformer_conv_rglru_trunk"


def rms_norm(x, g):
    xf = x.astype(jnp.float32)
    y = xf * lax.rsqrt(jnp.mean(xf * xf, axis=-1, keepdims=True) + RMS_EPS)
    return (y * g.astype(jnp.float32)).astype(x.dtype)


def layer_norm(x, g, b):
    xf = x.astype(jnp.float32)
    mu = jnp.mean(xf, axis=-1, keepdims=True)
    xc = xf - mu
    var = jnp.mean(xc * xc, axis=-1, keepdims=True)
    y = xc * lax.rsqrt(var + LN_EPS)
    return (y * g.astype(jnp.float32) + b.astype(jnp.float32)).astype(x.dtype)


def causal_depthwise_conv(x, w, b):
    k = w.shape[0]
    out = lax.conv_general_dilated(
        x, w[:, None, :].astype(x.dtype), window_strides=(1,), padding=[(k - 1, 0)],
        dimension_numbers=("NWC", "WIO", "NWC"), feature_group_count=x.shape[-1])
    return out + b.astype(x.dtype)


def block_diag_linear(x, w, b):
    bsz, s, _ = x.shape
    xh = x.reshape(bsz, s, LRU_HEADS, LRU_HEAD_DIM)
    y = jnp.einsum("bshi,hij->bshj", xh, w.astype(x.dtype))
    return y.reshape(bsz, s, D_LRU) + b.astype(x.dtype)


def conformer_conv_branch(v, gate, dw_w, dw_b, ln_g, ln_b):
    c = v * jax.nn.sigmoid(gate)
    c = causal_depthwise_conv(c, dw_w, dw_b)
    c = layer_norm(c, ln_g, ln_b)
    return jax.nn.silu(c)


def rglru_branch(xl, conv_w, conv_b, wa, ba, wx, bx, lam):
    xc = causal_depthwise_conv(xl, conv_w, conv_b)
    xf = xc.astype(jnp.float32)
    r = jax.nn.sigmoid(block_diag_linear(xc, wa, ba).astype(jnp.float32))
    i = jax.nn.sigmoid(block_diag_linear(xc, wx, bx).astype(jnp.float32))
    log_a = LRU_C * r * jax.nn.log_sigmoid(lam.astype(jnp.float32))
    a = jnp.exp(log_a)
    mult = jnp.sqrt(-jnp.expm1(2.0 * log_a))
    bterm = mult * (i * xf)

    def combine(left, right):
        a1, b1 = left
        a2, b2 = right
        return a1 * a2, a2 * b1 + b2

    _, h = lax.associative_scan(combine, (a, bterm), axis=1)
    return h.astype(xl.dtype)


def setup_inputs(seed: int = 0) -> dict:
    key = jax.random.key(seed)
    ks = jax.random.split(key, 20)
    f32 = jnp.float32
    x = jax.random.normal(ks[0], (BATCH, SEQ, D_MODEL), f32)
    norm_g = 1.0 + 0.02 * jax.random.normal(ks[1], (DEPTH, D_MODEL), f32)
    w_in = jax.random.normal(ks[2], (DEPTH, D_MODEL, D_IN), f32) * D_MODEL ** -0.5
    conv_dw_w = jax.random.normal(ks[3], (DEPTH, CONV_WIDTH, D_CONV), f32) * CONV_WIDTH ** -0.5
    conv_dw_b = 0.02 * jax.random.normal(ks[4], (DEPTH, D_CONV), f32)
    conv_ln_g = 1.0 + 0.02 * jax.random.normal(ks[5], (DEPTH, D_CONV), f32)
    conv_ln_b = 0.02 * jax.random.normal(ks[6], (DEPTH, D_CONV), f32)
    lru_conv_w = jax.random.normal(ks[7], (DEPTH, LRU_CONV_WIDTH, D_LRU), f32) * LRU_CONV_WIDTH ** -0.5
    lru_conv_b = 0.02 * jax.random.normal(ks[8], (DEPTH, D_LRU), f32)
    lru_wa = jax.random.normal(ks[9], (DEPTH, LRU_HEADS, LRU_HEAD_DIM, LRU_HEAD_DIM), f32) * LRU_HEAD_DIM ** -0.5
    lru_ba = 0.02 * jax.random.normal(ks[10], (DEPTH, D_LRU), f32)
    lru_wx = jax.random.normal(ks[11], (DEPTH, LRU_HEADS, LRU_HEAD_DIM, LRU_HEAD_DIM), f32) * LRU_HEAD_DIM ** -0.5
    lru_bx = 0.02 * jax.random.normal(ks[12], (DEPTH, D_LRU), f32)
    a_c = jax.random.uniform(ks[13], (DEPTH, D_LRU), f32, 0.9, 0.999)
    p = a_c ** (1.0 / LRU_C)
    lru_lambda = jnp.log(p) - jnp.log1p(-p)
    w_out = jax.random.normal(ks[14], (DEPTH, D_MIX, D_MODEL), f32) * D_MIX ** -0.5
    final_g = 1.0 + 0.02 * jax.random.normal(ks[15], (D_MODEL,), f32)
    return {"x": x, "norm_g": norm_g, "w_in": w_in,
            "conv_dw_w": conv_dw_w, "conv_dw_b": conv_dw_b, "conv_ln_g": conv_ln_g, "conv_ln_b": conv_ln_b,
            "lru_conv_w": lru_conv_w, "lru_conv_b": lru_conv_b, "lru_wa": lru_wa, "lru_ba": lru_ba,
            "lru_wx": lru_wx, "lru_bx": lru_bx, "lru_lambda": lru_lambda,
            "w_out": w_out, "final_g": final_g}


def reference(x, norm_g, w_in, conv_dw_w, conv_dw_b, conv_ln_g, conv_ln_b,
              lru_conv_w, lru_conv_b, lru_wa, lru_ba, lru_wx, lru_bx, lru_lambda,
              w_out, final_g):
    splits = [D_CONV, 2 * D_CONV, 3 * D_CONV, 3 * D_CONV + D_LRU]
    for l in range(DEPTH):
        h = rms_norm(x, norm_g[l])
        u = jnp.einsum("bsd,de->bse", h, w_in[l].astype(h.dtype))
        glu_v, glu_g, z_conv, x_lru, z_lru = jnp.split(u, splits, axis=-1)
        y_conv = conformer_conv_branch(glu_v, glu_g, conv_dw_w[l], conv_dw_b[l],
                                       conv_ln_g[l], conv_ln_b[l]) * jax.nn.silu(z_conv)
        y_lru = rglru_branch(x_lru, lru_conv_w[l], lru_conv_b[l], lru_wa[l], lru_ba[l],
                             lru_wx[l], lru_bx[l], lru_lambda[l]) * jax.nn.silu(z_lru)
        y = jnp.concatenate([y_conv, y_lru], axis=-1)
        x = x + jnp.einsum("bse,ed->bsd", y, w_out[l].astype(y.dtype))
    return rms_norm(x, final_g)
```

```python
import functools

import jax
import jax.numpy as jnp
from jax import lax
from jax.experimental import pallas as pl
from jax.experimental.pallas import tpu as pltpu

D_MODEL = 2048
SEQ = 8192
DEPTH = 4
D_CONV = 1024
D_LRU = 1024
CONV_WIDTH = 31
LRU_CONV_WIDTH = 4
LRU_HEADS = 8
LRU_HEAD_DIM = 128
LRU_C = 8.0
D_IN = 3 * D_CONV + 2 * D_LRU
RMS_EPS = 1e-6
LN_EPS = 1e-5

V7X_SUBLANES = 8
V7X_LANES = 128
V7X_VMEM_LIMIT_BYTES = 58 * 1024 * 1024

TILE = 256
HIST = 32
LHIST = V7X_SUBLANES
SCAN_BLOCK = V7X_SUBLANES * V7X_SUBLANES
MM_COLS = 512
CONV_ROWS = 64
LN_ROWS = 32

assert HIST >= CONV_WIDTH - 1 and HIST % V7X_SUBLANES == 0
assert LHIST >= LRU_CONV_WIDTH - 1
assert TILE % SCAN_BLOCK == 0 and TILE % CONV_ROWS == 0 and TILE % LN_ROWS == 0 and SEQ % TILE == 0


def _sigmoid(x):
    return 0.5 * jnp.tanh(0.5 * x) + 0.5


def _silu(x):
    return x * _sigmoid(x)


def _sublane_scan(p, h):
    sub = lax.broadcasted_iota(jnp.int32, p.shape, 0)
    for d in (1, 2, 4):
        p_prev = pltpu.roll(p, d, 0)
        h_prev = pltpu.roll(h, d, 0)
        keep = sub >= d
        h = h + p * jnp.where(keep, h_prev, 0.0)
        p = p * jnp.where(keep, p_prev, 1.0)
    return p, h


def _layer_kernel(x_ref, ng_ref, w_in_ref, dww_ref, dwb_ref, lng_ref, lnb_ref, lcw_ref, lcb_ref,
                  wg_ref, ba_ref, bx_ref, lam_ref, w_out_ref, fg_ref,
                  o_ref,
                  hbf, cbuf, convo, szc, xbuf, xcb, szl, a_s, b_s, hcar, ybf, *, final):
    f32 = jnp.float32
    bf16 = jnp.bfloat16
    i = pl.program_id(0)

    @pl.when(i == 0)
    def _():
        cbuf[0:HIST, :] = jnp.zeros((HIST, D_CONV), f32)
        xbuf[0:LHIST, :] = jnp.zeros((LHIST, D_LRU), f32)
        hcar[...] = jnp.zeros(hcar.shape, f32)

    x = x_ref[...]
    ms = jnp.mean(x * x, axis=-1, keepdims=True)
    hbf[...] = (x * lax.rsqrt(ms + RMS_EPS) * ng_ref[...]).astype(bf16)

    def in_proj(c0):
        return jnp.dot(hbf[...], w_in_ref[:, c0:c0 + MM_COLS], preferred_element_type=f32)

    for c0 in range(0, D_CONV, MM_COLS):
        v = in_proj(c0)
        g = in_proj(D_CONV + c0)
        cbuf[HIST:HIST + TILE, c0:c0 + MM_COLS] = v * _sigmoid(g)
    for c0 in range(0, D_CONV, MM_COLS):
        szc[:, c0:c0 + MM_COLS] = _silu(in_proj(2 * D_CONV + c0))
    for c0 in range(0, D_LRU, MM_COLS):
        xbuf[LHIST:LHIST + TILE, c0:c0 + MM_COLS] = in_proj(3 * D_CONV + c0)
    for c0 in range(0, D_LRU, MM_COLS):
        szl[:, c0:c0 + MM_COLS] = _silu(in_proj(3 * D_CONV + D_LRU + c0))

    for r0 in range(0, TILE, CONV_ROWS):
        for lt in range(D_CONV // V7X_LANES):
            ls = slice(lt * V7X_LANES, (lt + 1) * V7X_LANES)
            acc = jnp.broadcast_to(dwb_ref[:, ls], (CONV_ROWS, V7X_LANES))
            for k in range(CONV_WIDTH):
                off = r0 + HIST - (CONV_WIDTH - 1) + k
                acc = acc + dww_ref[k:k + 1, ls] * cbuf[off:off + CONV_ROWS, ls]
            convo[r0:r0 + CONV_ROWS, ls] = acc

    def ln_block(rb, carry):
        r0 = pl.multiple_of(rb * LN_ROWS, LN_ROWS)
        cv = convo[pl.ds(r0, LN_ROWS), :]
        mu = jnp.mean(cv, axis=-1, keepdims=True)
        xc = cv - mu
        var = jnp.mean(xc * xc, axis=-1, keepdims=True)
        yn = xc * lax.rsqrt(var + LN_EPS) * lng_ref[...] + lnb_ref[...]
        ybf[pl.ds(r0, LN_ROWS), 0:D_CONV] = (_silu(yn) * szc[pl.ds(r0, LN_ROWS), :]).astype(bf16)
        return carry

    lax.fori_loop(0, TILE // LN_ROWS, ln_block, 0)

    for r0 in range(0, TILE, LN_ROWS):
        acc = jnp.broadcast_to(lcb_ref[...], (LN_ROWS, D_LRU))
        for k in range(LRU_CONV_WIDTH):
            off = r0 + LHIST - (LRU_CONV_WIDTH - 1) + k
            acc = acc + lcw_ref[k:k + 1, :] * xbuf[off:off + LN_ROWS, :]
        xcb[r0:r0 + LN_ROWS, :] = acc

    lam = lam_ref[...]
    c_logsig = LRU_C * (jnp.minimum(lam, 0.0) - jnp.log1p(jnp.exp(-jnp.abs(lam))))

    for hd in range(LRU_HEADS):
        ls = slice(hd * LRU_HEAD_DIM, (hd + 1) * LRU_HEAD_DIM)
        xc = xcb[:, ls]
        pre = jnp.dot(xc.astype(bf16), wg_ref[hd], preferred_element_type=f32)
        r = _sigmoid(pre[:, :LRU_HEAD_DIM] + ba_ref[:, ls])
        ig = _sigmoid(pre[:, LRU_HEAD_DIM:] + bx_ref[:, ls])
        a = jnp.exp(c_logsig[:, ls] * r)
        a_s[hd] = a
        b_s[hd] = jnp.sqrt(1.0 - a * a) * (ig * xc)

    sub = lax.broadcasted_iota(jnp.int32, (V7X_SUBLANES, V7X_LANES), 0)
    for hd in range(LRU_HEADS):
        ls = slice(hd * LRU_HEAD_DIM, (hd + 1) * LRU_HEAD_DIM)
        cin = hcar[:, ls]
        for r0 in range(0, TILE, SCAN_BLOCK):
            hs, ps = [], []
            for j in range(V7X_SUBLANES):
                aj = a_s[hd, pl.ds(r0 + j, V7X_SUBLANES, stride=V7X_SUBLANES), :]
                bj = b_s[hd, pl.ds(r0 + j, V7X_SUBLANES, stride=V7X_SUBLANES), :]
                if j == 0:
                    hs.append(bj)
                    ps.append(aj)
                else:
                    hs.append(aj * hs[-1] + bj)
                    ps.append(aj * ps[-1])
            pc, hc = _sublane_scan(ps[-1], hs[-1])
            seg_end = hc + pc * cin
            seg_in = jnp.where(sub >= 1, pltpu.roll(seg_end, 1, 0), cin)
            for j in range(V7X_SUBLANES):
                a_s[hd, pl.ds(r0 + j, V7X_SUBLANES, stride=V7X_SUBLANES), :] = hs[j] + ps[j] * seg_in
            cin = jnp.broadcast_to(seg_end[V7X_SUBLANES - 1:V7X_SUBLANES, :], (V7X_SUBLANES, V7X_LANES))
        hcar[:, ls] = cin
        ybf[:, D_CONV + hd * LRU_HEAD_DIM:D_CONV + (hd + 1) * LRU_HEAD_DIM] = (a_s[hd] * szl[:, ls]).astype(bf16)

    for c0 in range(0, D_MODEL, MM_COLS):
        o_ref[:, c0:c0 + MM_COLS] = x_ref[:, c0:c0 + MM_COLS] + jnp.dot(
            ybf[...], w_out_ref[:, c0:c0 + MM_COLS], preferred_element_type=f32)

    if final:
        xo = o_ref[...]
        mso = jnp.mean(xo * xo, axis=-1, keepdims=True)
        o_ref[...] = xo * lax.rsqrt(mso + RMS_EPS) * fg_ref[...]

    cbuf[0:HIST, :] = cbuf[TILE:TILE + HIST, :]
    xbuf[0:LHIST, :] = xbuf[TILE:TILE + LHIST, :]


def _layer_call(x2d, layer, stacked, final_g, *, final):
    (norm_g, w_in, dww, dwb, lng, lnb, lcw, lcb, wg, ba, bx, lam, w_out) = stacked

    def per_layer(arr, single_buffer=False):
        block = (None,) + arr.shape[1:]
        nd = arr.ndim - 1
        index_map = lambda i: (layer,) + (0,) * nd
        if single_buffer:
            return pl.BlockSpec(block, index_map, pipeline_mode=pl.Buffered(1))
        return pl.BlockSpec(block, index_map)

    in_specs = [
        pl.BlockSpec((TILE, D_MODEL), lambda i: (i, 0)),
        per_layer(norm_g), per_layer(w_in, True), per_layer(dww), per_layer(dwb), per_layer(lng),
        per_layer(lnb), per_layer(lcw), per_layer(lcb), per_layer(wg, True), per_layer(ba),
        per_layer(bx), per_layer(lam), per_layer(w_out, True),
        pl.BlockSpec((1, D_MODEL), lambda i: (0, 0)),
    ]
    f32 = jnp.float32
    scratch = [
        pltpu.VMEM((TILE, D_MODEL), jnp.bfloat16),
        pltpu.VMEM((HIST + TILE, D_CONV), f32),
        pltpu.VMEM((TILE, D_CONV), f32),
        pltpu.VMEM((TILE, D_CONV), f32),
        pltpu.VMEM((LHIST + TILE, D_LRU), f32),
        pltpu.VMEM((TILE, D_LRU), f32),
        pltpu.VMEM((TILE, D_LRU), f32),
        pltpu.VMEM((LRU_HEADS, TILE, LRU_HEAD_DIM), f32),
        pltpu.VMEM((LRU_HEADS, TILE, LRU_HEAD_DIM), f32),
        pltpu.VMEM((V7X_SUBLANES, D_LRU), f32),
        pltpu.VMEM((TILE, D_CONV + D_LRU), jnp.bfloat16),
    ]
    return pl.pallas_call(
        functools.partial(_layer_kernel, final=final),
        grid=(SEQ // TILE,),
        in_specs=in_specs,
        out_specs=pl.BlockSpec((TILE, D_MODEL), lambda i: (i, 0)),
        out_shape=jax.ShapeDtypeStruct((SEQ, D_MODEL), f32),
        scratch_shapes=scratch,
        compiler_params=pltpu.CompilerParams(
            dimension_semantics=("arbitrary",), vmem_limit_bytes=V7X_VMEM_LIMIT_BYTES),
        name=f"trunk_layer{layer}",
    )(x2d, *stacked, final_g)


def kernel(x, norm_g, w_in, conv_dw_w, conv_dw_b, conv_ln_g, conv_ln_b, lru_conv_w, lru_conv_b,
           lru_wa, lru_ba, lru_wx, lru_bx, lru_lambda, w_out, final_g):
    bf16 = jnp.bfloat16
    row = lambda p: p.reshape(DEPTH, 1, p.shape[-1])
    stacked = (
        row(norm_g), w_in.astype(bf16), conv_dw_w, row(conv_dw_b), row(conv_ln_g), row(conv_ln_b),
        lru_conv_w, row(lru_conv_b),
        jnp.concatenate([lru_wa, lru_wx], axis=-1).astype(bf16),
        row(lru_ba), row(lru_bx), row(lru_lambda), w_out.astype(bf16),
    )
    x2d = x.reshape(SEQ, D_MODEL)
    fg = final_g.reshape(1, D_MODEL)
    for layer in range(DEPTH):
        x2d = _layer_call(x2d, layer, stacked, fg, final=(layer == DEPTH - 1))
    return x2d.reshape(x.shape)
```

```python
import functools

import jax
import jax.numpy as jnp
from jax import lax
from jax.experimental import pallas as pl
from jax.experimental.pallas import tpu as pltpu

D_MODEL = 2048
SEQ = 8192
DEPTH = 4
D_CONV = 1024
D_LRU = 1024
CONV_WIDTH = 31
LRU_CONV_WIDTH = 4
LRU_HEADS = 8
LRU_HEAD_DIM = 128
LRU_C = 8.0
D_IN = 3 * D_CONV + 2 * D_LRU
D_MIX = D_CONV + D_LRU
RMS_EPS = 1e-6
LN_EPS = 1e-5

V7X_SUBLANES = 8
V7X_LANES = 128
V7X_VMEM_LIMIT_BYTES = 58 * 1024 * 1024

TILE = 256
HIST = 32
LHIST = V7X_SUBLANES
SCAN_BLOCK = V7X_SUBLANES * V7X_SUBLANES
MM_COLS = 512
CONV_ROWS = 64
LN_ROWS = 32
W_CHUNK = 64

P_NORM_G = 0
P_DWB, P_LNG, P_LNB, P_LCB, P_BA, P_BX, P_LAM = 2, 3, 4, 5, 6, 7, 8
P_LCW = 9
P_FINAL_G = 13
P_DWW = 16
P_ROWS = 48

assert HIST >= CONV_WIDTH - 1 and HIST % V7X_SUBLANES == 0
assert LHIST >= LRU_CONV_WIDTH - 1
assert TILE % SCAN_BLOCK == 0 and TILE % CONV_ROWS == 0 and TILE % LN_ROWS == 0 and SEQ % TILE == 0
assert D_MODEL % W_CHUNK == 0 and D_MIX % W_CHUNK == 0 and P_DWW + CONV_WIDTH <= P_ROWS


def _sigmoid(x):
    return 0.5 * jnp.tanh(0.5 * x) + 0.5


def _silu(x):
    return x * _sigmoid(x)


def _rows_ahead(win, m):
    if m == 0:
        return win
    return pltpu.roll(win, win.shape[0] - m, 0)


def _sublane_scan(p, h):
    sub = lax.broadcasted_iota(jnp.int32, p.shape, 0)
    for d in (1, 2, 4):
        p_prev = pltpu.roll(p, d, 0)
        h_prev = pltpu.roll(h, d, 0)
        keep = sub >= d
        h = h + p * jnp.where(keep, h_prev, 0.0)
        p = p * jnp.where(keep, p_prev, 1.0)
    return p, h


def _load_weight(w_hbm, layer, stage, sem, dst):
    n_chunks = dst.shape[0] // W_CHUNK

    def chunk_copy(k, slot):
        return pltpu.make_async_copy(
            w_hbm.at[layer, pl.ds(k * W_CHUNK, W_CHUNK), :], stage.at[slot], sem.at[slot])

    chunk_copy(0, 0).start()

    def body(k, carry):
        slot = lax.rem(k, 2)

        @pl.when(k + 1 < n_chunks)
        def _():
            chunk_copy(k + 1, 1 - slot).start()

        chunk_copy(k, slot).wait()
        r0 = pl.multiple_of(k * W_CHUNK, W_CHUNK)
        dst[pl.ds(r0, W_CHUNK), :] = stage[slot].astype(dst.dtype)
        return carry

    lax.fori_loop(0, n_chunks, body, 0)


def _layer_kernel(x_ref, p_ref, w_in_hbm, wa_ref, wx_ref, w_out_hbm,
                  o_ref,
                  w_in_s, w_out_s, wg_s, stg_in, stg_out, sem_in, sem_out,
                  hbf, cbuf, convo, szc, xbuf, xcb, szl, a_s, b_s, hcar, ybf, *, layer, final):
    f32 = jnp.float32
    bf16 = jnp.bfloat16
    i = pl.program_id(0)

    @pl.when(i == 0)
    def _():
        _load_weight(w_in_hbm, layer, stg_in, sem_in, w_in_s)
        _load_weight(w_out_hbm, layer, stg_out, sem_out, w_out_s)
        for hd in range(LRU_HEADS):
            wg_s[hd, :, 0:LRU_HEAD_DIM] = wa_ref[hd].astype(bf16)
            wg_s[hd, :, LRU_HEAD_DIM:2 * LRU_HEAD_DIM] = wx_ref[hd].astype(bf16)
        cbuf[0:HIST, :] = jnp.zeros((HIST, D_CONV), f32)
        xbuf[0:LHIST, :] = jnp.zeros((LHIST, D_LRU), f32)
        hcar[...] = jnp.zeros(hcar.shape, f32)

    def prow(r, n=1):
        return p_ref[r:r + n, :]

    def prow2(r):
        return jnp.concatenate([p_ref[r:r + 1, :], p_ref[r + 1:r + 2, :]], axis=1)

    x = x_ref[...]
    ms = jnp.mean(x * x, axis=-1, keepdims=True)
    hbf[...] = (x * lax.rsqrt(ms + RMS_EPS) * prow2(P_NORM_G)).astype(bf16)

    def in_proj(c0):
        return jnp.dot(hbf[...], w_in_s[:, c0:c0 + MM_COLS], preferred_element_type=f32)

    for c0 in range(0, D_CONV, MM_COLS):
        v = in_proj(c0)
        g = in_proj(D_CONV + c0)
        cbuf[HIST:HIST + TILE, c0:c0 + MM_COLS] = v * _sigmoid(g)
    for c0 in range(0, D_CONV, MM_COLS):
        szc[:, c0:c0 + MM_COLS] = _silu(in_proj(2 * D_CONV + c0))
    for c0 in range(0, D_LRU, MM_COLS):
        xbuf[LHIST:LHIST + TILE, c0:c0 + MM_COLS] = in_proj(3 * D_CONV + c0)
    for c0 in range(0, D_LRU, MM_COLS):
        szl[:, c0:c0 + MM_COLS] = _silu(in_proj(3 * D_CONV + D_LRU + c0))

    lead = HIST - (CONV_WIDTH - 1)
    for r0 in range(0, TILE, CONV_ROWS):
        for lt in range(D_CONV // V7X_LANES):
            ls = slice(lt * V7X_LANES, (lt + 1) * V7X_LANES)
            win = cbuf[r0:r0 + CONV_ROWS + HIST, ls]
            acc = jnp.broadcast_to(p_ref[P_DWB:P_DWB + 1, ls], (CONV_ROWS, V7X_LANES))
            for m in range(V7X_SUBLANES):
                shifted = _rows_ahead(win, m)
                for k in range(CONV_WIDTH):
                    if (lead + k) % V7X_SUBLANES == m:
                        base = lead + k - m
                        acc = acc + p_ref[P_DWW + k:P_DWW + k + 1, ls] * shifted[base:base + CONV_ROWS]
            convo[r0:r0 + CONV_ROWS, ls] = acc

    def ln_block(rb, carry):
        r0 = pl.multiple_of(rb * LN_ROWS, LN_ROWS)
        cv = convo[pl.ds(r0, LN_ROWS), :]
        mu = jnp.mean(cv, axis=-1, keepdims=True)
        xc = cv - mu
        var = jnp.mean(xc * xc, axis=-1, keepdims=True)
        yn = xc * lax.rsqrt(var + LN_EPS) * prow(P_LNG) + prow(P_LNB)
        ybf[pl.ds(r0, LN_ROWS), 0:D_CONV] = (_silu(yn) * szc[pl.ds(r0, LN_ROWS), :]).astype(bf16)
        return carry

    lax.fori_loop(0, TILE // LN_ROWS, ln_block, 0)

    llead = LHIST - (LRU_CONV_WIDTH - 1)
    for r0 in range(0, TILE, LN_ROWS):
        win = xbuf[r0:r0 + LN_ROWS + LHIST, :]
        acc = jnp.broadcast_to(prow(P_LCB), (LN_ROWS, D_LRU))
        for k in range(LRU_CONV_WIDTH):
            m = (llead + k) % V7X_SUBLANES
            base = llead + k - m
            acc = acc + prow(P_LCW + k) * _rows_ahead(win, m)[base:base + LN_ROWS]
        xcb[r0:r0 + LN_ROWS, :] = acc

    lam = prow(P_LAM)
    c_logsig = LRU_C * (jnp.minimum(lam, 0.0) - jnp.log1p(jnp.exp(-jnp.abs(lam))))

    for hd in range(LRU_HEADS):
        ls = slice(hd * LRU_HEAD_DIM, (hd + 1) * LRU_HEAD_DIM)
        xc = xcb[:, ls]
        pre = jnp.dot(xc.astype(bf16), wg_s[hd], preferred_element_type=f32)
        r = _sigmoid(pre[:, :LRU_HEAD_DIM] + p_ref[P_BA:P_BA + 1, ls])
        ig = _sigmoid(pre[:, LRU_HEAD_DIM:] + p_ref[P_BX:P_BX + 1, ls])
        a = jnp.exp(c_logsig[:, ls] * r)
        a_s[hd] = a
        b_s[hd] = jnp.sqrt(1.0 - a * a) * (ig * xc)

    sub = lax.broadcasted_iota(jnp.int32, (V7X_SUBLANES, V7X_LANES), 0)
    for hd in range(LRU_HEADS):
        ls = slice(hd * LRU_HEAD_DIM, (hd + 1) * LRU_HEAD_DIM)
        cin = hcar[:, ls]
        for r0 in range(0, TILE, SCAN_BLOCK):
            hs, ps = [], []
            for j in range(V7X_SUBLANES):
                aj = a_s[hd, pl.ds(r0 + j, V7X_SUBLANES, stride=V7X_SUBLANES), :]
                bj = b_s[hd, pl.ds(r0 + j, V7X_SUBLANES, stride=V7X_SUBLANES), :]
                if j == 0:
                    hs.append(bj)
                    ps.append(aj)
                else:
                    hs.append(aj * hs[-1] + bj)
                    ps.append(aj * ps[-1])
            pc, hc = _sublane_scan(ps[-1], hs[-1])
            seg_end = hc + pc * cin
            seg_in = jnp.where(sub >= 1, pltpu.roll(seg_end, 1, 0), cin)
            for j in range(V7X_SUBLANES):
                a_s[hd, pl.ds(r0 + j, V7X_SUBLANES, stride=V7X_SUBLANES), :] = hs[j] + ps[j] * seg_in
            cin = jnp.broadcast_to(seg_end[V7X_SUBLANES - 1:V7X_SUBLANES, :], (V7X_SUBLANES, V7X_LANES))
        hcar[:, ls] = cin
        ybf[:, D_CONV + hd * LRU_HEAD_DIM:D_CONV + (hd + 1) * LRU_HEAD_DIM] = (a_s[hd] * szl[:, ls]).astype(bf16)

    for c0 in range(0, D_MODEL, MM_COLS):
        o_ref[:, c0:c0 + MM_COLS] = x_ref[:, c0:c0 + MM_COLS] + jnp.dot(
            ybf[...], w_out_s[:, c0:c0 + MM_COLS], preferred_element_type=f32)

    if final:
        xo = o_ref[...]
        mso = jnp.mean(xo * xo, axis=-1, keepdims=True)
        o_ref[...] = xo * lax.rsqrt(mso + RMS_EPS) * prow2(P_FINAL_G)

    cbuf[0:HIST, :] = cbuf[TILE:TILE + HIST, :]
    xbuf[0:LHIST, :] = xbuf[TILE:TILE + LHIST, :]


def _layer_call(x2d, layer, params, w_in, lru_wa, lru_wx, w_out, *, final):
    f32 = jnp.float32
    bf16 = jnp.bfloat16

    def per_layer(arr):
        nd = arr.ndim - 1
        return pl.BlockSpec((None,) + arr.shape[1:], lambda i: (layer,) + (0,) * nd)

    in_specs = [
        pl.BlockSpec((TILE, D_MODEL), lambda i: (i, 0)),
        per_layer(params),
        pl.BlockSpec(memory_space=pl.ANY),
        per_layer(lru_wa), per_layer(lru_wx),
        pl.BlockSpec(memory_space=pl.ANY),
    ]
    scratch = [
        pltpu.VMEM((D_MODEL, D_IN), bf16),
        pltpu.VMEM((D_MIX, D_MODEL), bf16),
        pltpu.VMEM((LRU_HEADS, LRU_HEAD_DIM, 2 * LRU_HEAD_DIM), bf16),
        pltpu.VMEM((2, W_CHUNK, D_IN), f32),
        pltpu.VMEM((2, W_CHUNK, D_MODEL), f32),
        pltpu.SemaphoreType.DMA((2,)),
        pltpu.SemaphoreType.DMA((2,)),
        pltpu.VMEM((TILE, D_MODEL), bf16),
        pltpu.VMEM((HIST + TILE, D_CONV), f32),
        pltpu.VMEM((TILE, D_CONV), f32),
        pltpu.VMEM((TILE, D_CONV), f32),
        pltpu.VMEM((LHIST + TILE, D_LRU), f32),
        pltpu.VMEM((TILE, D_LRU), f32),
        pltpu.VMEM((TILE, D_LRU), f32),
        pltpu.VMEM((LRU_HEADS, TILE, LRU_HEAD_DIM), f32),
        pltpu.VMEM((LRU_HEADS, TILE, LRU_HEAD_DIM), f32),
        pltpu.VMEM((V7X_SUBLANES, D_LRU), f32),
        pltpu.VMEM((TILE, D_MIX), bf16),
    ]
    return pl.pallas_call(
        functools.partial(_layer_kernel, layer=layer, final=final),
        grid=(SEQ // TILE,),
        in_specs=in_specs,
        out_specs=pl.BlockSpec((TILE, D_MODEL), lambda i: (i, 0)),
        out_shape=jax.ShapeDtypeStruct((SEQ, D_MODEL), f32),
        scratch_shapes=scratch,
        compiler_params=pltpu.CompilerParams(
            dimension_semantics=("arbitrary",), vmem_limit_bytes=V7X_VMEM_LIMIT_BYTES),
        name=f"trunk_layer{layer}",
    )(x2d, params, w_in, lru_wa, lru_wx, w_out)


def _pack_params(norm_g, conv_dw_w, conv_dw_b, conv_ln_g, conv_ln_b, lru_conv_w, lru_conv_b,
                 lru_ba, lru_bx, lru_lambda, final_g):
    row = lambda p: p[:, None, :]
    fg = jnp.broadcast_to(final_g.reshape(1, 2, D_MODEL // 2), (DEPTH, 2, D_MODEL // 2))
    pad = lambda n: jnp.zeros((DEPTH, n, D_CONV), jnp.float32)
    return jnp.concatenate([
        norm_g.reshape(DEPTH, 2, D_MODEL // 2),
        row(conv_dw_b), row(conv_ln_g), row(conv_ln_b), row(lru_conv_b), row(lru_ba), row(lru_bx),
        row(lru_lambda), lru_conv_w, fg, pad(P_DWW - P_FINAL_G - 2), conv_dw_w,
        pad(P_ROWS - P_DWW - CONV_WIDTH)], axis=1)


def kernel(x, norm_g, w_in, conv_dw_w, conv_dw_b, conv_ln_g, conv_ln_b, lru_conv_w, lru_conv_b,
           lru_wa, lru_ba, lru_wx, lru_bx, lru_lambda, w_out, final_g):
    params = _pack_params(norm_g, conv_dw_w, conv_dw_b, conv_ln_g, conv_ln_b, lru_conv_w, lru_conv_b,
                          lru_ba, lru_bx, lru_lambda, final_g)
    x2d = x.reshape(SEQ, D_MODEL)
    for layer in range(DEPTH):
        x2d = _layer_call(x2d, layer, params, w_in, lru_wa, lru_wx, w_out, final=(layer == DEPTH - 1))
    return x2d.reshape(x.shape)
```

```python
import functools

import jax
import jax.numpy as jnp
from jax import lax
from jax.experimental import pallas as pl
from jax.experimental.pallas import tpu as pltpu

D_MODEL = 2048
SEQ = 8192
DEPTH = 4
D_CONV = 1024
D_LRU = 1024
CONV_WIDTH = 31
LRU_CONV_WIDTH = 4
LRU_HEADS = 8
LRU_HEAD_DIM = 128
LRU_C = 8.0
D_IN = 3 * D_CONV + 2 * D_LRU
D_MIX = D_CONV + D_LRU
RMS_EPS = 1e-6
LN_EPS = 1e-5

V7X_SUBLANES = 8
V7X_LANES = 128
V7X_VMEM_LIMIT_BYTES = 58 * 1024 * 1024

TILE = 256
HIST = 32
LHIST = V7X_SUBLANES
SCAN_BLOCK = V7X_SUBLANES * V7X_SUBLANES
MM_COLS = 512
CONV_ROWS = 64
LN_ROWS = 32
NORM_ROWS = 16
W_CHUNK = 16
W_SLOTS = 8

P_NORM_G = 0
P_DWB, P_LNG, P_LNB, P_LCB, P_BA, P_BX, P_LAM = 2, 3, 4, 5, 6, 7, 8
P_LCW = 9
P_FINAL_G = 13
P_DWW = 16
P_ROWS = 48

assert HIST >= CONV_WIDTH - 1 and HIST % V7X_SUBLANES == 0
assert LHIST >= LRU_CONV_WIDTH - 1
assert TILE % SCAN_BLOCK == 0 and TILE % CONV_ROWS == 0 and TILE % LN_ROWS == 0 and SEQ % TILE == 0
assert D_MODEL % W_CHUNK == 0 and D_MIX % W_CHUNK == 0 and P_DWW + CONV_WIDTH <= P_ROWS
assert D_MODEL // W_CHUNK >= W_SLOTS and D_MIX // W_CHUNK >= W_SLOTS


def _sigmoid(x):
    return 0.5 * jnp.tanh(0.5 * x) + 0.5


def _silu(x):
    return x * _sigmoid(x)


def _rows_ahead(win, m):
    if m == 0:
        return win
    return pltpu.roll(win, win.shape[0] - m, 0)


def _sublane_scan(p, h):
    sub = lax.broadcasted_iota(jnp.int32, p.shape, 0)
    for d in (1, 2, 4):
        p_prev = pltpu.roll(p, d, 0)
        h_prev = pltpu.roll(h, d, 0)
        keep = sub >= d
        h = h + p * jnp.where(keep, h_prev, 0.0)
        p = p * jnp.where(keep, p_prev, 1.0)
    return p, h


def _weight_chunk_copy(w_hbm, layer, stage, sem, k):
    slot = lax.rem(k, W_SLOTS)
    return pltpu.make_async_copy(
        w_hbm.at[layer, pl.ds(k * W_CHUNK, W_CHUNK), :], stage.at[slot], sem.at[slot])


def _start_weight_load(w_hbm, layer, stage, sem):
    for k in range(W_SLOTS):
        _weight_chunk_copy(w_hbm, layer, stage, sem, k).start()


def _finish_weight_load(w_hbm, layer, stage, sem, dst):
    n_chunks = dst.shape[0] // W_CHUNK

    def body(k, carry):
        _weight_chunk_copy(w_hbm, layer, stage, sem, k).wait()
        r0 = pl.multiple_of(k * W_CHUNK, W_CHUNK)
        dst[pl.ds(r0, W_CHUNK), :] = stage[lax.rem(k, W_SLOTS)].astype(dst.dtype)

        @pl.when(k + W_SLOTS < n_chunks)
        def _():
            _weight_chunk_copy(w_hbm, layer, stage, sem, k + W_SLOTS).start()

        return carry

    lax.fori_loop(0, n_chunks, body, 0)


def _layer_kernel(x_ref, p_ref, w_in_hbm, wa_ref, wx_ref, w_out_hbm,
                  o_ref,
                  w_in_s, w_out_s, wg_s, stg_in, stg_out, sem_in, sem_out,
                  hbf, cbuf, convo, szc, xbuf, xcb, szl, a_s, b_s, hcar, ybf, *, layer, final):
    f32 = jnp.float32
    bf16 = jnp.bfloat16
    i = pl.program_id(0)

    @pl.when(i == 0)
    def _():
        _start_weight_load(w_in_hbm, layer, stg_in, sem_in)
        _start_weight_load(w_out_hbm, layer, stg_out, sem_out)
        _finish_weight_load(w_in_hbm, layer, stg_in, sem_in, w_in_s)
        _finish_weight_load(w_out_hbm, layer, stg_out, sem_out, w_out_s)
        for hd in range(LRU_HEADS):
            wg_s[hd, :, 0:LRU_HEAD_DIM] = wa_ref[hd].astype(bf16)
            wg_s[hd, :, LRU_HEAD_DIM:2 * LRU_HEAD_DIM] = wx_ref[hd].astype(bf16)
        cbuf[0:HIST, :] = jnp.zeros((HIST, D_CONV), f32)
        xbuf[0:LHIST, :] = jnp.zeros((LHIST, D_LRU), f32)
        hcar[...] = jnp.zeros(hcar.shape, f32)

    def prow(r, n=1):
        return p_ref[r:r + n, :]

    def prow2(r):
        return jnp.concatenate([p_ref[r:r + 1, :], p_ref[r + 1:r + 2, :]], axis=1)

    zrow = pl.multiple_of(jnp.minimum(i, 0) * NORM_ROWS, NORM_ROWS)
    norm_g = prow2(P_NORM_G)
    for r0 in range(0, TILE, NORM_ROWS):
        x = x_ref[r0:r0 + NORM_ROWS, :]
        ms = jnp.mean(x * x, axis=-1, keepdims=True)
        hbf[pl.ds(r0 + zrow, NORM_ROWS), :] = (x * lax.rsqrt(ms + RMS_EPS) * norm_g).astype(bf16)

    def in_proj(c0):
        return jnp.dot(hbf[...], w_in_s[:, c0:c0 + MM_COLS], preferred_element_type=f32)

    for c0 in range(0, D_CONV, MM_COLS):
        v = in_proj(c0)
        g = in_proj(D_CONV + c0)
        cbuf[HIST:HIST + TILE, c0:c0 + MM_COLS] = v * _sigmoid(g)
    for c0 in range(0, D_CONV, MM_COLS):
        szc[:, c0:c0 + MM_COLS] = _silu(in_proj(2 * D_CONV + c0))
    for c0 in range(0, D_LRU, MM_COLS):
        xbuf[LHIST:LHIST + TILE, c0:c0 + MM_COLS] = in_proj(3 * D_CONV + c0)
    for c0 in range(0, D_LRU, MM_COLS):
        szl[:, c0:c0 + MM_COLS] = _silu(in_proj(3 * D_CONV + D_LRU + c0))

    lead = HIST - (CONV_WIDTH - 1)
    for r0 in range(0, TILE, CONV_ROWS):
        for lt in range(D_CONV // V7X_LANES):
            ls = slice(lt * V7X_LANES, (lt + 1) * V7X_LANES)
            win = cbuf[r0:r0 + CONV_ROWS + HIST, ls]
            acc = jnp.broadcast_to(p_ref[P_DWB:P_DWB + 1, ls], (CONV_ROWS, V7X_LANES))
            for m in range(V7X_SUBLANES):
                rows = CONV_ROWS if m == 0 else CONV_ROWS + V7X_SUBLANES
                partial = None
                for k in range(CONV_WIDTH):
                    if (lead + k) % V7X_SUBLANES == m:
                        base = lead + k - m
                        term = p_ref[P_DWW + k:P_DWW + k + 1, ls] * win[base:base + rows]
                        partial = term if partial is None else partial + term
                acc = acc + _rows_ahead(partial, m)[0:CONV_ROWS]
            convo[r0:r0 + CONV_ROWS, ls] = acc

    def ln_block(rb, carry):
        r0 = pl.multiple_of(rb * LN_ROWS, LN_ROWS)
        cv = convo[pl.ds(r0, LN_ROWS), :]
        mu = jnp.mean(cv, axis=-1, keepdims=True)
        xc = cv - mu
        var = jnp.mean(xc * xc, axis=-1, keepdims=True)
        yn = xc * lax.rsqrt(var + LN_EPS) * prow(P_LNG) + prow(P_LNB)
        ybf[pl.ds(r0, LN_ROWS), 0:D_CONV] = (_silu(yn) * szc[pl.ds(r0, LN_ROWS), :]).astype(bf16)
        return carry

    lax.fori_loop(0, TILE // LN_ROWS, ln_block, 0)

    llead = LHIST - (LRU_CONV_WIDTH - 1)
    for r0 in range(0, TILE, LN_ROWS):
        win = xbuf[r0:r0 + LN_ROWS + LHIST, :]
        acc = jnp.broadcast_to(prow(P_LCB), (LN_ROWS, D_LRU))
        for k in range(LRU_CONV_WIDTH):
            m = (llead + k) % V7X_SUBLANES
            base = llead + k - m
            acc = acc + prow(P_LCW + k) * _rows_ahead(win, m)[base:base + LN_ROWS]
        xcb[r0:r0 + LN_ROWS, :] = acc

    lam = prow(P_LAM)
    c_logsig = LRU_C * (jnp.minimum(lam, 0.0) - jnp.log1p(jnp.exp(-jnp.abs(lam))))

    for hd in range(LRU_HEADS):
        ls = slice(hd * LRU_HEAD_DIM, (hd + 1) * LRU_HEAD_DIM)
        xc = xcb[:, ls]
        pre = jnp.dot(xc.astype(bf16), wg_s[hd], preferred_element_type=f32)
        r = _sigmoid(pre[:, :LRU_HEAD_DIM] + p_ref[P_BA:P_BA + 1, ls])
        ig = _sigmoid(pre[:, LRU_HEAD_DIM:] + p_ref[P_BX:P_BX + 1, ls])
        a = jnp.exp(c_logsig[:, ls] * r)
        a_s[hd] = a
        b_s[hd] = jnp.sqrt(1.0 - a * a) * (ig * xc)

    sub = lax.broadcasted_iota(jnp.int32, (V7X_SUBLANES, V7X_LANES), 0)
    for hd in range(LRU_HEADS):
        ls = slice(hd * LRU_HEAD_DIM, (hd + 1) * LRU_HEAD_DIM)
        cin = hcar[:, ls]
        for r0 in range(0, TILE, SCAN_BLOCK):
            hs, ps = [], []
            for j in range(V7X_SUBLANES):
                aj = a_s[hd, pl.ds(r0 + j, V7X_SUBLANES, stride=V7X_SUBLANES), :]
                bj = b_s[hd, pl.ds(r0 + j, V7X_SUBLANES, stride=V7X_SUBLANES), :]
                if j == 0:
                    hs.append(bj)
                    ps.append(aj)
                else:
                    hs.append(aj * hs[-1] + bj)
                    ps.append(aj * ps[-1])
            pc, hc = _sublane_scan(ps[-1], hs[-1])
            seg_end = hc + pc * cin
            seg_in = jnp.where(sub >= 1, pltpu.roll(seg_end, 1, 0), cin)
            for j in range(V7X_SUBLANES):
                a_s[hd, pl.ds(r0 + j, V7X_SUBLANES, stride=V7X_SUBLANES), :] = hs[j] + ps[j] * seg_in
            cin = jnp.broadcast_to(seg_end[V7X_SUBLANES - 1:V7X_SUBLANES, :], (V7X_SUBLANES, V7X_LANES))
        hcar[:, ls] = cin
        ybf[:, D_CONV + hd * LRU_HEAD_DIM:D_CONV + (hd + 1) * LRU_HEAD_DIM] = (a_s[hd] * szl[:, ls]).astype(bf16)

    for c0 in range(0, D_MODEL, MM_COLS):
        o_ref[:, c0:c0 + MM_COLS] = x_ref[:, c0:c0 + MM_COLS] + jnp.dot(
            ybf[...], w_out_s[:, c0:c0 + MM_COLS], preferred_element_type=f32)

    if final:
        xo = o_ref[...]
        mso = jnp.mean(xo * xo, axis=-1, keepdims=True)
        o_ref[...] = xo * lax.rsqrt(mso + RMS_EPS) * prow2(P_FINAL_G)

    cbuf[0:HIST, :] = cbuf[TILE:TILE + HIST, :]
    xbuf[0:LHIST, :] = xbuf[TILE:TILE + LHIST, :]


def _layer_call(x2d, layer, params, w_in, lru_wa, lru_wx, w_out, *, final):
    f32 = jnp.float32
    bf16 = jnp.bfloat16

    def per_layer(arr):
        nd = arr.ndim - 1
        return pl.BlockSpec((None,) + arr.shape[1:], lambda i: (layer,) + (0,) * nd)

    in_specs = [
        pl.BlockSpec((TILE, D_MODEL), lambda i: (i, 0)),
        per_layer(params),
        pl.BlockSpec(memory_space=pl.ANY),
        per_layer(lru_wa), per_layer(lru_wx),
        pl.BlockSpec(memory_space=pl.ANY),
    ]
    scratch = [
        pltpu.VMEM((D_MODEL, D_IN), bf16),
        pltpu.VMEM((D_MIX, D_MODEL), bf16),
        pltpu.VMEM((LRU_HEADS, LRU_HEAD_DIM, 2 * LRU_HEAD_DIM), bf16),
        pltpu.VMEM((W_SLOTS, W_CHUNK, D_IN), f32),
        pltpu.VMEM((W_SLOTS, W_CHUNK, D_MODEL), f32),
        pltpu.SemaphoreType.DMA((W_SLOTS,)),
        pltpu.SemaphoreType.DMA((W_SLOTS,)),
        pltpu.VMEM((TILE, D_MODEL), bf16),
        pltpu.VMEM((HIST + TILE, D_CONV), f32),
        pltpu.VMEM((TILE, D_CONV), f32),
        pltpu.VMEM((TILE, D_CONV), f32),
        pltpu.VMEM((LHIST + TILE, D_LRU), f32),
        pltpu.VMEM((TILE, D_LRU), f32),
        pltpu.VMEM((TILE, D_LRU), f32),
        pltpu.VMEM((LRU_HEADS, TILE, LRU_HEAD_DIM), f32),
        pltpu.VMEM((LRU_HEADS, TILE, LRU_HEAD_DIM), f32),
        pltpu.VMEM((V7X_SUBLANES, D_LRU), f32),
        pltpu.VMEM((TILE, D_MIX), bf16),
    ]
    return pl.pallas_call(
        functools.partial(_layer_kernel, layer=layer, final=final),
        grid=(SEQ // TILE,),
        in_specs=in_specs,
        out_specs=pl.BlockSpec((TILE, D_MODEL), lambda i: (i, 0)),
        out_shape=jax.ShapeDtypeStruct((SEQ, D_MODEL), f32),
        scratch_shapes=scratch,
        compiler_params=pltpu.CompilerParams(
            dimension_semantics=("arbitrary",), vmem_limit_bytes=V7X_VMEM_LIMIT_BYTES),
        name=f"trunk_layer{layer}",
    )(x2d, params, w_in, lru_wa, lru_wx, w_out)


def _pack_params(norm_g, conv_dw_w, conv_dw_b, conv_ln_g, conv_ln_b, lru_conv_w, lru_conv_b,
                 lru_ba, lru_bx, lru_lambda, final_g):
    row = lambda p: p[:, None, :]
    fg = jnp.broadcast_to(final_g.reshape(1, 2, D_MODEL // 2), (DEPTH, 2, D_MODEL // 2))
    pad = lambda n: jnp.zeros((DEPTH, n, D_CONV), jnp.float32)
    return jnp.concatenate([
        norm_g.reshape(DEPTH, 2, D_MODEL // 2),
        row(conv_dw_b), row(conv_ln_g), row(conv_ln_b), row(lru_conv_b), row(lru_ba), row(lru_bx),
        row(lru_lambda), lru_conv_w, fg, pad(P_DWW - P_FINAL_G - 2), conv_dw_w,
        pad(P_ROWS - P_DWW - CONV_WIDTH)], axis=1)


def kernel(x, norm_g, w_in, conv_dw_w, conv_dw_b, conv_ln_g, conv_ln_b, lru_conv_w, lru_conv_b,
           lru_wa, lru_ba, lru_wx, lru_bx, lru_lambda, w_out, final_g):
    params = _pack_params(norm_g, conv_dw_w, conv_dw_b, conv_ln_g, conv_ln_b, lru_conv_w, lru_conv_b,
                          lru_ba, lru_bx, lru_lambda, final_g)
    x2d = x.reshape(SEQ, D_MODEL)
    for layer in range(DEPTH):
        x2d = _layer_call(x2d, layer, params, w_in, lru_wa, lru_wx, w_out, final=(layer == DEPTH - 1))
    return x2d.reshape(x.shape)
```

```python
import functools

import jax
import jax.numpy as jnp
from jax import lax
from jax.experimental import pallas as pl
from jax.experimental.pallas import tpu as pltpu

D_MODEL = 2048
SEQ = 8192
DEPTH = 4
D_CONV = 1024
D_LRU = 1024
CONV_WIDTH = 31
LRU_CONV_WIDTH = 4
LRU_HEADS = 8
LRU_HEAD_DIM = 128
LRU_C = 8.0
D_IN = 3 * D_CONV + 2 * D_LRU
D_MIX = D_CONV + D_LRU
RMS_EPS = 1e-6
LN_EPS = 1e-5

V7X_SUBLANES = 8
V7X_LANES = 128
V7X_VMEM_LIMIT_BYTES = 58 * 1024 * 1024

TILE = 256
HIST = 32
LHIST = V7X_SUBLANES
SCAN_BLOCK = V7X_SUBLANES * V7X_SUBLANES
MM_COLS = 512
CONV_ROWS = 64
LN_ROWS = 32
NORM_ROWS = 16
W_CHUNK = 16
W_SLOTS = 8

P_NORM_G = 0
P_DWB, P_LNG, P_LNB, P_LCB, P_BA, P_BX, P_LAM = 2, 3, 4, 5, 6, 7, 8
P_LCW = 9
P_FINAL_G = 13
P_DWW = 16
P_ROWS = 48

assert HIST >= CONV_WIDTH - 1 and HIST % V7X_SUBLANES == 0
assert LHIST >= LRU_CONV_WIDTH - 1
assert TILE % SCAN_BLOCK == 0 and TILE % CONV_ROWS == 0 and TILE % LN_ROWS == 0 and SEQ % TILE == 0
assert D_MODEL % W_CHUNK == 0 and D_MIX % W_CHUNK == 0 and P_DWW + CONV_WIDTH <= P_ROWS
assert D_MODEL // W_CHUNK >= W_SLOTS and D_MIX // W_CHUNK >= W_SLOTS


def _sigmoid(x):
    return 0.5 * jnp.tanh(0.5 * x) + 0.5


def _silu(x):
    return x * _sigmoid(x)


def _rows_ahead(win, m):
    if m == 0:
        return win
    return pltpu.roll(win, win.shape[0] - m, 0)


def _sublane_scan(p, h):
    sub = lax.broadcasted_iota(jnp.int32, p.shape, 0)
    for d in (1, 2, 4):
        p_prev = pltpu.roll(p, d, 0)
        h_prev = pltpu.roll(h, d, 0)
        keep = sub >= d
        h = h + p * jnp.where(keep, h_prev, 0.0)
        p = p * jnp.where(keep, p_prev, 1.0)
    return p, h


def _weight_chunk_copy(w_hbm, layer, stage, sem, k):
    slot = lax.rem(k, W_SLOTS)
    return pltpu.make_async_copy(
        w_hbm.at[layer, pl.ds(k * W_CHUNK, W_CHUNK), :], stage.at[slot], sem.at[slot])


def _start_weight_load(w_hbm, layer, stage, sem):
    for k in range(W_SLOTS):
        _weight_chunk_copy(w_hbm, layer, stage, sem, k).start()


def _finish_weight_load(w_hbm, layer, stage, sem, dst):
    n_chunks = dst.shape[0] // W_CHUNK

    def body(k, carry):
        _weight_chunk_copy(w_hbm, layer, stage, sem, k).wait()
        r0 = pl.multiple_of(k * W_CHUNK, W_CHUNK)
        dst[pl.ds(r0, W_CHUNK), :] = stage[lax.rem(k, W_SLOTS)].astype(dst.dtype)

        @pl.when(k + W_SLOTS < n_chunks)
        def _():
            _weight_chunk_copy(w_hbm, layer, stage, sem, k + W_SLOTS).start()

        return carry

    lax.fori_loop(0, n_chunks, body, 0)


def _layer_kernel(x_ref, p_ref, w_in_hbm, wa_ref, wx_ref, w_out_hbm,
                  o_ref,
                  w_in_s, w_out_s, wg_s, stg_in, stg_out, sem_in, sem_out,
                  hbf, cbuf, convo, szc, xbuf, xcb, szl, a_s, b_s, hcar, ybf, *, layer, final):
    f32 = jnp.float32
    bf16 = jnp.bfloat16
    i = pl.program_id(0)

    @pl.when(i == 0)
    def _():
        _start_weight_load(w_in_hbm, layer, stg_in, sem_in)
        _start_weight_load(w_out_hbm, layer, stg_out, sem_out)
        _finish_weight_load(w_in_hbm, layer, stg_in, sem_in, w_in_s)
        _finish_weight_load(w_out_hbm, layer, stg_out, sem_out, w_out_s)
        for hd in range(LRU_HEADS):
            wg_s[hd, :, 0:LRU_HEAD_DIM] = wa_ref[hd].astype(bf16)
            wg_s[hd, :, LRU_HEAD_DIM:2 * LRU_HEAD_DIM] = wx_ref[hd].astype(bf16)
        cbuf[0:HIST, :] = jnp.zeros((HIST, D_CONV), f32)
        xbuf[0:LHIST, :] = jnp.zeros((LHIST, D_LRU), f32)
        hcar[...] = jnp.zeros(hcar.shape, f32)

    def prow(r, n=1):
        return p_ref[r:r + n, :]

    def prow2(r):
        return jnp.concatenate([p_ref[r:r + 1, :], p_ref[r + 1:r + 2, :]], axis=1)

    zrow = pl.multiple_of(jnp.minimum(i, 0) * NORM_ROWS, NORM_ROWS)
    norm_g = prow2(P_NORM_G)
    for r0 in range(0, TILE, NORM_ROWS):
        x = x_ref[r0:r0 + NORM_ROWS, :]
        ms = jnp.mean(x * x, axis=-1, keepdims=True)
        hbf[pl.ds(r0 + zrow, NORM_ROWS), :] = (x * lax.rsqrt(ms + RMS_EPS) * norm_g).astype(bf16)

    def in_proj(c0):
        return jnp.dot(hbf[...], w_in_s[:, c0:c0 + MM_COLS], preferred_element_type=f32)

    def glu_chunk(c0):
        v = in_proj(c0)
        g = in_proj(D_CONV + c0)
        cbuf[HIST:HIST + TILE, c0:c0 + MM_COLS] = v * _sigmoid(g)

    def zc_chunk(c0):
        szc[:, c0:c0 + MM_COLS] = _silu(in_proj(2 * D_CONV + c0))

    def xl_chunk(c0):
        xbuf[LHIST:LHIST + TILE, c0:c0 + MM_COLS] = in_proj(3 * D_CONV + c0)

    def zl_chunk(c0):
        szl[:, c0:c0 + MM_COLS] = _silu(in_proj(3 * D_CONV + D_LRU + c0))

    lead = HIST - (CONV_WIDTH - 1)

    def conv_block(r0):
        for lt in range(D_CONV // V7X_LANES):
            ls = slice(lt * V7X_LANES, (lt + 1) * V7X_LANES)
            win = cbuf[r0:r0 + CONV_ROWS + HIST, ls]
            acc = jnp.broadcast_to(p_ref[P_DWB:P_DWB + 1, ls], (CONV_ROWS, V7X_LANES))
            for m in range(V7X_SUBLANES):
                rows = CONV_ROWS if m == 0 else CONV_ROWS + V7X_SUBLANES
                partial = None
                for k in range(CONV_WIDTH):
                    if (lead + k) % V7X_SUBLANES == m:
                        base = lead + k - m
                        term = p_ref[P_DWW + k:P_DWW + k + 1, ls] * win[base:base + rows]
                        partial = term if partial is None else partial + term
                acc = acc + _rows_ahead(partial, m)[0:CONV_ROWS]
            convo[r0:r0 + CONV_ROWS, ls] = acc

    def ln_block(r0):
        cv = convo[r0:r0 + LN_ROWS, :]
        mu = jnp.mean(cv, axis=-1, keepdims=True)
        xc = cv - mu
        var = jnp.mean(xc * xc, axis=-1, keepdims=True)
        yn = xc * lax.rsqrt(var + LN_EPS) * prow(P_LNG) + prow(P_LNB)
        ybf[r0:r0 + LN_ROWS, 0:D_CONV] = (_silu(yn) * szc[r0:r0 + LN_ROWS, :]).astype(bf16)

    llead = LHIST - (LRU_CONV_WIDTH - 1)

    def lconv_block(r0):
        win = xbuf[r0:r0 + LN_ROWS + LHIST, :]
        acc = jnp.broadcast_to(prow(P_LCB), (LN_ROWS, D_LRU))
        for k in range(LRU_CONV_WIDTH):
            m = (llead + k) % V7X_SUBLANES
            base = llead + k - m
            acc = acc + prow(P_LCW + k) * _rows_ahead(win, m)[base:base + LN_ROWS]
        xcb[r0:r0 + LN_ROWS, :] = acc

    lam = prow(P_LAM)
    c_logsig = LRU_C * (jnp.minimum(lam, 0.0) - jnp.log1p(jnp.exp(-jnp.abs(lam))))

    def gate_head(hd):
        ls = slice(hd * LRU_HEAD_DIM, (hd + 1) * LRU_HEAD_DIM)
        xc = xcb[:, ls]
        pre = jnp.dot(xc.astype(bf16), wg_s[hd], preferred_element_type=f32)
        r = _sigmoid(pre[:, :LRU_HEAD_DIM] + p_ref[P_BA:P_BA + 1, ls])
        ig = _sigmoid(pre[:, LRU_HEAD_DIM:] + p_ref[P_BX:P_BX + 1, ls])
        a = jnp.exp(c_logsig[:, ls] * r)
        a_s[hd] = a
        b_s[hd] = jnp.sqrt(1.0 - a * a) * (ig * xc)

    sub = lax.broadcasted_iota(jnp.int32, (V7X_SUBLANES, V7X_LANES), 0)

    def scan_head(hd):
        ls = slice(hd * LRU_HEAD_DIM, (hd + 1) * LRU_HEAD_DIM)
        cin = hcar[:, ls]
        for r0 in range(0, TILE, SCAN_BLOCK):
            hs, ps = [], []
            for j in range(V7X_SUBLANES):
                aj = a_s[hd, pl.ds(r0 + j, V7X_SUBLANES, stride=V7X_SUBLANES), :]
                bj = b_s[hd, pl.ds(r0 + j, V7X_SUBLANES, stride=V7X_SUBLANES), :]
                if j == 0:
                    hs.append(bj)
                    ps.append(aj)
                else:
                    hs.append(aj * hs[-1] + bj)
                    ps.append(aj * ps[-1])
            pc, hc = _sublane_scan(ps[-1], hs[-1])
            seg_end = hc + pc * cin
            seg_in = jnp.where(sub >= 1, pltpu.roll(seg_end, 1, 0), cin)
            for j in range(V7X_SUBLANES):
                a_s[hd, pl.ds(r0 + j, V7X_SUBLANES, stride=V7X_SUBLANES), :] = hs[j] + ps[j] * seg_in
            cin = jnp.broadcast_to(seg_end[V7X_SUBLANES - 1:V7X_SUBLANES, :], (V7X_SUBLANES, V7X_LANES))
        hcar[:, ls] = cin
        ybf[:, D_CONV + hd * LRU_HEAD_DIM:D_CONV + (hd + 1) * LRU_HEAD_DIM] = (a_s[hd] * szl[:, ls]).astype(bf16)

    def out_conv_half(c0):
        o_ref[:, c0:c0 + MM_COLS] = x_ref[:, c0:c0 + MM_COLS] + jnp.dot(
            ybf[:, 0:D_CONV], w_out_s[0:D_CONV, c0:c0 + MM_COLS], preferred_element_type=f32)

    def out_lru_half(c0):
        o_ref[:, c0:c0 + MM_COLS] += jnp.dot(
            ybf[:, D_CONV:D_MIX], w_out_s[D_CONV:D_MIX, c0:c0 + MM_COLS], preferred_element_type=f32)

    conv_starts = list(range(0, TILE, CONV_ROWS))
    col_starts = list(range(0, D_CONV, MM_COLS))
    out_starts = list(range(0, D_MODEL, MM_COLS))
    assert len(conv_starts) == 2 * len(col_starts) and len(out_starts) * 2 == LRU_HEADS

    for c0 in col_starts:
        glu_chunk(c0)
    for c0 in col_starts:
        xl_chunk(c0)
    for n, c0 in enumerate(col_starts):
        conv_block(conv_starts[2 * n])
        zc_chunk(c0)
        conv_block(conv_starts[2 * n + 1])
        zl_chunk(c0)
    for r0 in range(0, TILE, LN_ROWS):
        lconv_block(r0)
    for hd in range(LRU_HEADS):
        gate_head(hd)
    for r0 in range(0, TILE, LN_ROWS):
        ln_block(r0)
    for n, c0 in enumerate(out_starts):
        out_conv_half(c0)
        scan_head(2 * n)
        scan_head(2 * n + 1)
    for c0 in out_starts:
        out_lru_half(c0)

    if final:
        xo = o_ref[...]
        mso = jnp.mean(xo * xo, axis=-1, keepdims=True)
        o_ref[...] = xo * lax.rsqrt(mso + RMS_EPS) * prow2(P_FINAL_G)

    cbuf[0:HIST, :] = cbuf[TILE:TILE + HIST, :]
    xbuf[0:LHIST, :] = xbuf[TILE:TILE + LHIST, :]


def _layer_call(x2d, layer, params, w_in, lru_wa, lru_wx, w_out, *, final):
    f32 = jnp.float32
    bf16 = jnp.bfloat16

    def per_layer(arr):
        nd = arr.ndim - 1
        return pl.BlockSpec((None,) + arr.shape[1:], lambda i: (layer,) + (0,) * nd)

    in_specs = [
        pl.BlockSpec((TILE, D_MODEL), lambda i: (i, 0)),
        per_layer(params),
        pl.BlockSpec(memory_space=pl.ANY),
        per_layer(lru_wa), per_layer(lru_wx),
        pl.BlockSpec(memory_space=pl.ANY),
    ]
    scratch = [
        pltpu.VMEM((D_MODEL, D_IN), bf16),
        pltpu.VMEM((D_MIX, D_MODEL), bf16),
        pltpu.VMEM((LRU_HEADS, LRU_HEAD_DIM, 2 * LRU_HEAD_DIM), bf16),
        pltpu.VMEM((W_SLOTS, W_CHUNK, D_IN), f32),
        pltpu.VMEM((W_SLOTS, W_CHUNK, D_MODEL), f32),
        pltpu.SemaphoreType.DMA((W_SLOTS,)),
        pltpu.SemaphoreType.DMA((W_SLOTS,)),
        pltpu.VMEM((TILE, D_MODEL), bf16),
        pltpu.VMEM((HIST + TILE, D_CONV), f32),
        pltpu.VMEM((TILE, D_CONV), f32),
        pltpu.VMEM((TILE, D_CONV), f32),
        pltpu.VMEM((LHIST + TILE, D_LRU), f32),
        pltpu.VMEM((TILE, D_LRU), f32),
        pltpu.VMEM((TILE, D_LRU), f32),
        pltpu.VMEM((LRU_HEADS, TILE, LRU_HEAD_DIM), f32),
        pltpu.VMEM((LRU_HEADS, TILE, LRU_HEAD_DIM), f32),
        pltpu.VMEM((V7X_SUBLANES, D_LRU), f32),
        pltpu.VMEM((TILE, D_MIX), bf16),
    ]
    return pl.pallas_call(
        functools.partial(_layer_kernel, layer=layer, final=final),
        grid=(SEQ // TILE,),
        in_specs=in_specs,
        out_specs=pl.BlockSpec((TILE, D_MODEL), lambda i: (i, 0)),
        out_shape=jax.ShapeDtypeStruct((SEQ, D_MODEL), f32),
        scratch_shapes=scratch,
        compiler_params=pltpu.CompilerParams(
            dimension_semantics=("arbitrary",), vmem_limit_bytes=V7X_VMEM_LIMIT_BYTES),
        name=f"trunk_layer{layer}",
    )(x2d, params, w_in, lru_wa, lru_wx, w_out)


def _pack_params(norm_g, conv_dw_w, conv_dw_b, conv_ln_g, conv_ln_b, lru_conv_w, lru_conv_b,
                 lru_ba, lru_bx, lru_lambda, final_g):
    row = lambda p: p[:, None, :]
    fg = jnp.broadcast_to(final_g.reshape(1, 2, D_MODEL // 2), (DEPTH, 2, D_MODEL // 2))
    pad = lambda n: jnp.zeros((DEPTH, n, D_CONV), jnp.float32)
    return jnp.concatenate([
        norm_g.reshape(DEPTH, 2, D_MODEL // 2),
        row(conv_dw_b), row(conv_ln_g), row(conv_ln_b), row(lru_conv_b), row(lru_ba), row(lru_bx),
        row(lru_lambda), lru_conv_w, fg, pad(P_DWW - P_FINAL_G - 2), conv_dw_w,
        pad(P_ROWS - P_DWW - CONV_WIDTH)], axis=1)


def kernel(x, norm_g, w_in, conv_dw_w, conv_dw_b, conv_ln_g, conv_ln_b, lru_conv_w, lru_conv_b,
           lru_wa, lru_ba, lru_wx, lru_bx, lru_lambda, w_out, final_g):
    params = _pack_params(norm_g, conv_dw_w, conv_dw_b, conv_ln_g, conv_ln_b, lru_conv_w, lru_conv_b,
                          lru_ba, lru_bx, lru_lambda, final_g)
    x2d = x.reshape(SEQ, D_MODEL)
    for layer in range(DEPTH):
        x2d = _layer_call(x2d, layer, params, w_in, lru_wa, lru_wx, w_out, final=(layer == DEPTH - 1))
    return x2d.reshape(x.shape)
```

```python
import functools

import jax
import jax.numpy as jnp
from jax import lax
from jax.experimental import pallas as pl
from jax.experimental.pallas import tpu as pltpu

D_MODEL = 2048
SEQ = 8192
DEPTH = 4
D_CONV = 1024
D_LRU = 1024
CONV_WIDTH = 31
LRU_CONV_WIDTH = 4
LRU_HEADS = 8
LRU_HEAD_DIM = 128
LRU_C = 8.0
D_IN = 3 * D_CONV + 2 * D_LRU
D_MIX = D_CONV + D_LRU
RMS_EPS = 1e-6
LN_EPS = 1e-5

V7X_SUBLANES = 8
V7X_LANES = 128
V7X_VMEM_LIMIT_BYTES = 58 * 1024 * 1024

TILE = 256
HIST = 32
LHIST = V7X_SUBLANES
SCAN_BLOCK = V7X_SUBLANES * V7X_SUBLANES
MM_COLS = 512
CONV_ROWS = 64
LN_ROWS = 32
GATE_ROWS = 128
NORM_ROWS = 16
W_CHUNK = 32
W_SLOTS = 8

P_NORM_G = 0
P_DWB, P_LNG, P_LNB, P_LCB, P_BA, P_BX, P_LAM = 2, 3, 4, 5, 6, 7, 8
P_LCW = 9
P_FINAL_G = 13
P_DWW = 16
P_ROWS = 48

assert HIST >= CONV_WIDTH - 1 and HIST % V7X_SUBLANES == 0
assert LHIST >= LRU_CONV_WIDTH - 1
assert TILE % SCAN_BLOCK == 0 and TILE % CONV_ROWS == 0 and TILE % LN_ROWS == 0 and SEQ % TILE == 0
assert TILE % GATE_ROWS == 0 and TILE % NORM_ROWS == 0
assert D_MODEL % W_CHUNK == 0 and D_MIX % W_CHUNK == 0 and P_DWW + CONV_WIDTH <= P_ROWS
assert D_MODEL // W_CHUNK >= W_SLOTS and D_MIX // W_CHUNK >= W_SLOTS


def _sigmoid(x):
    return 0.5 * jnp.tanh(0.5 * x) + 0.5


def _silu(x):
    h = 0.5 * x
    return h * jnp.tanh(h) + h


def _rows_ahead(win, m):
    if m == 0:
        return win
    return pltpu.roll(win, win.shape[0] - m, 0)


def _sublane_scan(p, h):
    sub = lax.broadcasted_iota(jnp.int32, p.shape, 0)
    for d in (1, 2, 4):
        p_prev = pltpu.roll(p, d, 0)
        h_prev = pltpu.roll(h, d, 0)
        keep = sub >= d
        h = h + p * jnp.where(keep, h_prev, 0.0)
        p = p * jnp.where(keep, p_prev, 1.0)
    return p, h


def _weight_chunk_copy(w_hbm, layer, stage, sem, k):
    slot = lax.rem(k, W_SLOTS)
    return pltpu.make_async_copy(
        w_hbm.at[layer, pl.ds(k * W_CHUNK, W_CHUNK), :], stage.at[slot], sem.at[slot])


def _start_weight_load(w_hbm, layer, stage, sem):
    for k in range(W_SLOTS):
        _weight_chunk_copy(w_hbm, layer, stage, sem, k).start()


def _finish_weight_load(w_hbm, layer, stage, sem, dst):
    n_chunks = dst.shape[0] // W_CHUNK

    def body(k, carry):
        _weight_chunk_copy(w_hbm, layer, stage, sem, k).wait()
        r0 = pl.multiple_of(k * W_CHUNK, W_CHUNK)
        dst[pl.ds(r0, W_CHUNK), :] = stage[lax.rem(k, W_SLOTS)].astype(dst.dtype)

        @pl.when(k + W_SLOTS < n_chunks)
        def _():
            _weight_chunk_copy(w_hbm, layer, stage, sem, k + W_SLOTS).start()

        return carry

    lax.fori_loop(0, n_chunks, body, 0)


def _layer_kernel(x_ref, p_ref, w_in_hbm, wa_ref, wx_ref, w_out_hbm,
                  o_ref,
                  w_in_s, w_out_s, wg_s, stg_in, stg_out, sem_in, sem_out,
                  hbf, cbuf, convo, szc, xbuf, szl, a_s, b_s, hcar, ybf, *, layer, final):
    f32 = jnp.float32
    bf16 = jnp.bfloat16
    i = pl.program_id(0)

    @pl.when(i == 0)
    def _():
        _start_weight_load(w_in_hbm, layer, stg_in, sem_in)
        _start_weight_load(w_out_hbm, layer, stg_out, sem_out)
        _finish_weight_load(w_in_hbm, layer, stg_in, sem_in, w_in_s)
        _finish_weight_load(w_out_hbm, layer, stg_out, sem_out, w_out_s)
        for hd in range(LRU_HEADS):
            wg_s[hd, :, 0:LRU_HEAD_DIM] = wa_ref[hd].astype(bf16)
            wg_s[hd, :, LRU_HEAD_DIM:2 * LRU_HEAD_DIM] = wx_ref[hd].astype(bf16)
        cbuf[0:HIST, :] = jnp.zeros((HIST, D_CONV), f32)
        xbuf[0:LHIST, :] = jnp.zeros((LHIST, D_LRU), f32)
        hcar[...] = jnp.zeros(hcar.shape, f32)

    def prow(r, n=1):
        return p_ref[r:r + n, :]

    def prow2(r):
        return jnp.concatenate([p_ref[r:r + 1, :], p_ref[r + 1:r + 2, :]], axis=1)

    zrow = pl.multiple_of(jnp.minimum(i, 0) * NORM_ROWS, NORM_ROWS)
    norm_g = prow2(P_NORM_G)
    for r0 in range(0, TILE, NORM_ROWS):
        x = x_ref[r0:r0 + NORM_ROWS, :]
        ms = jnp.mean(x * x, axis=-1, keepdims=True)
        hbf[pl.ds(r0 + zrow, NORM_ROWS), :] = (x * lax.rsqrt(ms + RMS_EPS) * norm_g).astype(bf16)

    def in_proj(c0):
        return jnp.dot(hbf[...], w_in_s[:, c0:c0 + MM_COLS], preferred_element_type=f32)

    def glu_chunk(c0):
        v = in_proj(c0)
        g = in_proj(D_CONV + c0)
        cbuf[HIST:HIST + TILE, c0:c0 + MM_COLS] = v * _sigmoid(g)

    def zc_chunk(c0):
        szc[:, c0:c0 + MM_COLS] = _silu(in_proj(2 * D_CONV + c0))

    def xl_chunk(c0):
        xbuf[LHIST:LHIST + TILE, c0:c0 + MM_COLS] = in_proj(3 * D_CONV + c0)

    def zl_chunk(c0):
        szl[:, c0:c0 + MM_COLS] = _silu(in_proj(3 * D_CONV + D_LRU + c0))

    lead = HIST - (CONV_WIDTH - 1)

    def conv_block(r0):
        for lt in range(D_CONV // V7X_LANES):
            ls = slice(lt * V7X_LANES, (lt + 1) * V7X_LANES)
            win = cbuf[r0:r0 + CONV_ROWS + HIST, ls]
            acc = jnp.broadcast_to(p_ref[P_DWB:P_DWB + 1, ls], (CONV_ROWS, V7X_LANES))
            for m in range(V7X_SUBLANES):
                rows = CONV_ROWS if m == 0 else CONV_ROWS + V7X_SUBLANES
                partial = None
                for k in range(CONV_WIDTH):
                    if (lead + k) % V7X_SUBLANES == m:
                        base = lead + k - m
                        term = p_ref[P_DWW + k:P_DWW + k + 1, ls] * win[base:base + rows]
                        partial = term if partial is None else partial + term
                acc = acc + _rows_ahead(partial, m)[0:CONV_ROWS]
            convo[r0:r0 + CONV_ROWS, ls] = acc

    def ln_block(r0):
        cv = convo[r0:r0 + LN_ROWS, :]
        mu = jnp.mean(cv, axis=-1, keepdims=True)
        xc = cv - mu
        var = jnp.mean(xc * xc, axis=-1, keepdims=True)
        yn = xc * lax.rsqrt(var + LN_EPS) * prow(P_LNG) + prow(P_LNB)
        ybf[r0:r0 + LN_ROWS, 0:D_CONV] = (_silu(yn) * szc[r0:r0 + LN_ROWS, :]).astype(bf16)

    llead = LHIST - (LRU_CONV_WIDTH - 1)

    lam = prow(P_LAM)
    c_logsig = LRU_C * (jnp.minimum(lam, 0.0) - jnp.log1p(jnp.exp(-jnp.abs(lam))))

    def gate_block(hd, r0):
        ls = slice(hd * LRU_HEAD_DIM, (hd + 1) * LRU_HEAD_DIM)
        win = xbuf[r0:r0 + GATE_ROWS + LHIST, ls]
        xc = jnp.broadcast_to(p_ref[P_LCB:P_LCB + 1, ls], (GATE_ROWS, LRU_HEAD_DIM))
        for k in range(LRU_CONV_WIDTH):
            m = (llead + k) % V7X_SUBLANES
            base = llead + k - m
            xc = xc + p_ref[P_LCW + k:P_LCW + k + 1, ls] * _rows_ahead(win, m)[base:base + GATE_ROWS]
        pre = jnp.dot(xc.astype(bf16), wg_s[hd], preferred_element_type=f32)
        r = _sigmoid(pre[:, :LRU_HEAD_DIM] + p_ref[P_BA:P_BA + 1, ls])
        ig = _sigmoid(pre[:, LRU_HEAD_DIM:] + p_ref[P_BX:P_BX + 1, ls])
        a = jnp.exp(c_logsig[:, ls] * r)
        a_s[hd, r0:r0 + GATE_ROWS, :] = a
        b_s[hd, r0:r0 + GATE_ROWS, :] = jnp.sqrt(1.0 - a * a) * (ig * xc)

    sub = lax.broadcasted_iota(jnp.int32, (V7X_SUBLANES, V7X_LANES), 0)

    def scan_head(hd):
        ls = slice(hd * LRU_HEAD_DIM, (hd + 1) * LRU_HEAD_DIM)
        cin = hcar[:, ls]
        for r0 in range(0, TILE, SCAN_BLOCK):
            hs, ps = [], []
            for j in range(V7X_SUBLANES):
                aj = a_s[hd, pl.ds(r0 + j, V7X_SUBLANES, stride=V7X_SUBLANES), :]
                bj = b_s[hd, pl.ds(r0 + j, V7X_SUBLANES, stride=V7X_SUBLANES), :]
                if j == 0:
                    hs.append(bj)
                    ps.append(aj)
                else:
                    hs.append(aj * hs[-1] + bj)
                    ps.append(aj * ps[-1])
            pc, hc = _sublane_scan(ps[-1], hs[-1])
            seg_end = hc + pc * cin
            seg_in = jnp.where(sub >= 1, pltpu.roll(seg_end, 1, 0), cin)
            for j in range(V7X_SUBLANES):
                a_s[hd, pl.ds(r0 + j, V7X_SUBLANES, stride=V7X_SUBLANES), :] = hs[j] + ps[j] * seg_in
            cin = jnp.broadcast_to(seg_end[V7X_SUBLANES - 1:V7X_SUBLANES, :], (V7X_SUBLANES, V7X_LANES))
        hcar[:, ls] = cin
        ybf[:, D_CONV + hd * LRU_HEAD_DIM:D_CONV + (hd + 1) * LRU_HEAD_DIM] = (a_s[hd] * szl[:, ls]).astype(bf16)

    def out_conv_half(c0):
        o_ref[:, c0:c0 + MM_COLS] = x_ref[:, c0:c0 + MM_COLS] + jnp.dot(
            ybf[:, 0:D_CONV], w_out_s[0:D_CONV, c0:c0 + MM_COLS], preferred_element_type=f32)

    def out_lru_half(c0):
        o_ref[:, c0:c0 + MM_COLS] += jnp.dot(
            ybf[:, D_CONV:D_MIX], w_out_s[D_CONV:D_MIX, c0:c0 + MM_COLS], preferred_element_type=f32)

    conv_starts = list(range(0, TILE, CONV_ROWS))
    col_starts = list(range(0, D_CONV, MM_COLS))
    out_starts = list(range(0, D_MODEL, MM_COLS))
    assert len(conv_starts) == 2 * len(col_starts) and len(out_starts) * 2 == LRU_HEADS

    for c0 in col_starts:
        glu_chunk(c0)
    for c0 in col_starts:
        xl_chunk(c0)
    for n, c0 in enumerate(col_starts):
        conv_block(conv_starts[2 * n])
        zc_chunk(c0)
        conv_block(conv_starts[2 * n + 1])
        zl_chunk(c0)
    for hd in range(LRU_HEADS):
        for r0 in range(0, TILE, GATE_ROWS):
            gate_block(hd, r0)
    for r0 in range(0, TILE, LN_ROWS):
        ln_block(r0)
    for n, c0 in enumerate(out_starts):
        out_conv_half(c0)
        scan_head(2 * n)
        scan_head(2 * n + 1)
    for c0 in out_starts:
        out_lru_half(c0)

    if final:
        xo = o_ref[...]
        mso = jnp.mean(xo * xo, axis=-1, keepdims=True)
        o_ref[...] = xo * lax.rsqrt(mso + RMS_EPS) * prow2(P_FINAL_G)

    cbuf[0:HIST, :] = cbuf[TILE:TILE + HIST, :]
    xbuf[0:LHIST, :] = xbuf[TILE:TILE + LHIST, :]


def _layer_call(x2d, layer, params, w_in, lru_wa, lru_wx, w_out, *, final):
    f32 = jnp.float32
    bf16 = jnp.bfloat16

    def per_layer(arr):
        nd = arr.ndim - 1
        return pl.BlockSpec((None,) + arr.shape[1:], lambda i: (layer,) + (0,) * nd)

    in_specs = [
        pl.BlockSpec((TILE, D_MODEL), lambda i: (i, 0)),
        per_layer(params),
        pl.BlockSpec(memory_space=pl.ANY),
        per_layer(lru_wa), per_layer(lru_wx),
        pl.BlockSpec(memory_space=pl.ANY),
    ]
    scratch = [
        pltpu.VMEM((D_MODEL, D_IN), bf16),
        pltpu.VMEM((D_MIX, D_MODEL), bf16),
        pltpu.VMEM((LRU_HEADS, LRU_HEAD_DIM, 2 * LRU_HEAD_DIM), bf16),
        pltpu.VMEM((W_SLOTS, W_CHUNK, D_IN), f32),
        pltpu.VMEM((W_SLOTS, W_CHUNK, D_MODEL), f32),
        pltpu.SemaphoreType.DMA((W_SLOTS,)),
        pltpu.SemaphoreType.DMA((W_SLOTS,)),
        pltpu.VMEM((TILE, D_MODEL), bf16),
        pltpu.VMEM((HIST + TILE, D_CONV), f32),
        pltpu.VMEM((TILE, D_CONV), f32),
        pltpu.VMEM((TILE, D_CONV), f32),
        pltpu.VMEM((LHIST + TILE, D_LRU), f32),
        pltpu.VMEM((TILE, D_LRU), f32),
        pltpu.VMEM((LRU_HEADS, TILE, LRU_HEAD_DIM), f32),
        pltpu.VMEM((LRU_HEADS, TILE, LRU_HEAD_DIM), f32),
        pltpu.VMEM((V7X_SUBLANES, D_LRU), f32),
        pltpu.VMEM((TILE, D_MIX), bf16),
    ]
    return pl.pallas_call(
        functools.partial(_layer_kernel, layer=layer, final=final),
        grid=(SEQ // TILE,),
        in_specs=in_specs,
        out_specs=pl.BlockSpec((TILE, D_MODEL), lambda i: (i, 0)),
        out_shape=jax.ShapeDtypeStruct((SEQ, D_MODEL), f32),
        scratch_shapes=scratch,
        compiler_params=pltpu.CompilerParams(
            dimension_semantics=("arbitrary",), vmem_limit_bytes=V7X_VMEM_LIMIT_BYTES),
        name=f"trunk_layer{layer}",
    )(x2d, params, w_in, lru_wa, lru_wx, w_out)


def _pack_params(norm_g, conv_dw_w, conv_dw_b, conv_ln_g, conv_ln_b, lru_conv_w, lru_conv_b,
                 lru_ba, lru_bx, lru_lambda, final_g):
    row = lambda p: p[:, None, :]
    fg = jnp.broadcast_to(final_g.reshape(1, 2, D_MODEL // 2), (DEPTH, 2, D_MODEL // 2))
    pad = lambda n: jnp.zeros((DEPTH, n, D_CONV), jnp.float32)
    return jnp.concatenate([
        norm_g.reshape(DEPTH, 2, D_MODEL // 2),
        row(conv_dw_b), row(conv_ln_g), row(conv_ln_b), row(lru_conv_b), row(lru_ba), row(lru_bx),
        row(lru_lambda), lru_conv_w, fg, pad(P_DWW - P_FINAL_G - 2), conv_dw_w,
        pad(P_ROWS - P_DWW - CONV_WIDTH)], axis=1)


def kernel(x, norm_g, w_in, conv_dw_w, conv_dw_b, conv_ln_g, conv_ln_b, lru_conv_w, lru_conv_b,
           lru_wa, lru_ba, lru_wx, lru_bx, lru_lambda, w_out, final_g):
    params = _pack_params(norm_g, conv_dw_w, conv_dw_b, conv_ln_g, conv_ln_b, lru_conv_w, lru_conv_b,
                          lru_ba, lru_bx, lru_lambda, final_g)
    x2d = x.reshape(SEQ, D_MODEL)
    for layer in range(DEPTH):
        x2d = _layer_call(x2d, layer, params, w_in, lru_wa, lru_wx, w_out, final=(layer == DEPTH - 1))
    return x2d.reshape(x.shape)
```
